```python
import jax, jax.numpy as jnp
from jax import lax
import numpy as np

D_MODEL = 1024
BATCH = 32
SEQ = 256
DEPTH = 4
DEC_BATCH = 2
DEC_SEQ = 4096
PAST_LEN = 512

GRID_W = 64
N_MIXERS = 2
N_ML = (DEPTH + 1) // 2
N_MLA = DEPTH // 2
D_FF = 2816
EPS = 1e-6
ML_DI = 2 * D_MODEL
ML_NH = 4
ML_DH = ML_DI // ML_NH
ML_CONV_K = 5
ML_QKV_BLOCK = 4
ML_CHUNK = 64
MLA_NH = 8
MLA_NOPE = 128
MLA_ROPE = 64
MLA_DV = 128
MLA_QK = MLA_NOPE + MLA_ROPE
MLA_Q_LORA = 256
MLA_KV_LORA = 128
ROPE_THETA = 10000.0
ATTN_QBLOCK = 128

kernel_name = 'hybrid_mlstm_mla_flow_trunk'


def rms_norm(x, g):
    xf = x.astype(jnp.float32)
    y = xf * lax.rsqrt(jnp.mean(xf * xf, axis=-1, keepdims=True) + EPS)
    return (y * g.astype(jnp.float32)).astype(x.dtype)


def modulate(x, g, shift, scale):
    return rms_norm(x, g) * (1 + scale) + shift


def swiglu(x, w_in, w_out):
    a, b = jnp.split(x @ w_in, 2, axis=-1)
    return (jax.nn.silu(a) * b) @ w_out


def trunk_layer(x, mod, g3, w_in2, w_out2, mixer):
    sh1, sc1, ga1, sh2, sc2, ga2, sh3, sc3, ga3 = jnp.split(mod, 9, axis=-1)
    x = x + 0.5 * ga1 * swiglu(modulate(x, g3[0], sh1, sc1), w_in2[0], w_out2[0])
    y, aux = mixer(modulate(x, g3[1], sh2, sc2))
    x = x + ga2 * y
    x = x + 0.5 * ga3 * swiglu(modulate(x, g3[2], sh3, sc3), w_in2[1], w_out2[1])
    return x, aux


def grid_rope(n_tokens):
    rows = n_tokens // GRID_W
    r = jnp.repeat(jnp.arange(rows, dtype=jnp.float32), GRID_W)
    col = jnp.tile(jnp.arange(GRID_W, dtype=jnp.float32), rows)
    n_f = MLA_ROPE // 4
    freqs = jnp.power(ROPE_THETA, -jnp.arange(n_f, dtype=jnp.float32) / n_f)
    ang = jnp.concatenate([r[:, None] * freqs, col[:, None] * freqs], axis=-1)
    return jnp.cos(ang), jnp.sin(ang)


def apply_rope(x, cos, sin):
    x1, x2 = jnp.split(x, 2, axis=-1)
    cs = cos[None, :, None, :].astype(x.dtype)
    sn = sin[None, :, None, :].astype(x.dtype)
    return jnp.concatenate([x1 * cs - x2 * sn, x1 * sn + x2 * cs], axis=-1)


def block_attention(q, k, v):
    B, Tq, H, dk = q.shape
    nb = Tq // ATTN_QBLOCK
    qb = jnp.moveaxis(q.reshape(B, nb, ATTN_QBLOCK, H, dk), 1, 0)
    scale = dk ** -0.5

    def one_block(qi):
        s = jnp.einsum('bqhd,bkhd->bhqk', qi, k).astype(jnp.float32) * scale
        p = jax.nn.softmax(s, axis=-1).astype(v.dtype)
        return jnp.einsum('bhqk,bkhd->bqhd', p, v)

    o = lax.map(one_block, qb)
    return jnp.moveaxis(o, 0, 1).reshape(B, Tq, H, v.shape[-1])


def mla_queries(cq, q_norm_g, w_uq, q_g, rope):
    B, T, _ = cq.shape
    q = (rms_norm(cq, q_norm_g) @ w_uq).reshape(B, T, MLA_NH, MLA_QK)
    q = rms_norm(q, q_g)
    if rope is not None:
        q = jnp.concatenate([q[..., :MLA_NOPE], apply_rope(q[..., MLA_NOPE:], *rope)], axis=-1)
    return q


def mla_entry(kvr, kv_norm_g):
    return jnp.concatenate([rms_norm(kvr[..., :MLA_KV_LORA], kv_norm_g), kvr[..., MLA_KV_LORA:]], axis=-1)


def mla_keys_values(entry, w_ukv, k_g, rope):
    B, T, _ = entry.shape
    kv = (entry[..., :MLA_KV_LORA] @ w_ukv).reshape(B, T, MLA_NH, MLA_NOPE + MLA_DV)
    kr = jnp.broadcast_to(entry[:, :, None, MLA_KV_LORA:], (B, T, MLA_NH, MLA_ROPE))
    k = rms_norm(jnp.concatenate([kv[..., :MLA_NOPE], kr], axis=-1), k_g)
    if rope is not None:
        k = jnp.concatenate([k[..., :MLA_NOPE], apply_rope(k[..., MLA_NOPE:], *rope)], axis=-1)
    return k, kv[..., MLA_NOPE:]


def mla_mixer(xn, w_in, q_norm_g, w_uq, kv_norm_g, w_ukv, qk_g, w_o, ctx_entry, rope):
    B, T, _ = xn.shape
    h = xn @ w_in
    q = mla_queries(h[..., :MLA_Q_LORA], q_norm_g, w_uq, qk_g[0], rope)
    entry = mla_entry(h[..., MLA_Q_LORA:], kv_norm_g)
    k, v = mla_keys_values(entry, w_ukv, qk_g[1], rope)
    if ctx_entry is not None:
        kc, vc = mla_keys_values(ctx_entry, w_ukv, qk_g[1], None)
        k = jnp.concatenate([k, kc], axis=1)
        v = jnp.concatenate([v, vc], axis=1)
    o = block_attention(q, k, v).reshape(B, T, MLA_NH * MLA_DV)
    return o @ w_o, entry


def centred_dwconv(x, w, b):
    kw = w.shape[0]
    y = lax.conv_general_dilated(x, w[:, None, :], window_strides=(1,), padding=[(kw // 2, kw // 2)],
                                 dimension_numbers=('NWC', 'WIO', 'NWC'), feature_group_count=x.shape[-1])
    return y + b


def headwise(x, w):
    B, T, _ = x.shape
    return jnp.einsum('btni,nio->btno', x.reshape(B, T, -1, ML_QKV_BLOCK), w).reshape(B, T, ML_DI)


def to_heads(x):
    B, T, _ = x.shape
    return x.reshape(B, T, ML_NH, ML_DH).transpose(0, 2, 1, 3).astype(jnp.float32)


def mlstm_scan(q, k, v, ig, lf, C0, n0, m0):
    B, H, T, DH = q.shape
    nc = T // ML_CHUNK

    def chunks(a):
        return jnp.moveaxis(a.reshape(a.shape[:2] + (nc, ML_CHUNK) + a.shape[3:]), 2, 0)

    causal = jnp.tril(jnp.ones((ML_CHUNK, ML_CHUNK), dtype=bool))

    def step(carry, inp):
        C, n, m = carry
        qc, kc, vc, ic, fc = inp
        b = jnp.cumsum(fc, axis=-1)
        dlog = jnp.where(causal, b[..., :, None] - b[..., None, :] + ic[..., None, :], -jnp.inf)
        inter = b + m[..., None]
        m_row = jnp.maximum(jnp.max(dlog, axis=-1), inter)
        s = jnp.einsum('bhjd,bhtd->bhjt', qc, kc) * jnp.exp(dlog - m_row[..., None])
        decay = jnp.exp(inter - m_row)
        num = decay[..., None] * jnp.einsum('bhvk,bhjk->bhjv', C, qc) + jnp.einsum('bhjt,bhtv->bhjv', s, vc)
        den = decay * jnp.einsum('bhk,bhjk->bhj', n, qc) + jnp.sum(s, axis=-1)
        h = num / jnp.maximum(jnp.abs(den), jnp.exp(-m_row))[..., None]
        b_last = b[..., -1]
        wlog = b_last[..., None] - b + ic
        m_new = jnp.maximum(b_last + m, jnp.max(wlog, axis=-1))
        w = jnp.exp(wlog - m_new[..., None])
        carry_decay = jnp.exp(b_last + m - m_new)
        C_new = carry_decay[..., None, None] * C + jnp.einsum('bhtv,bhtk->bhvk', vc * w[..., None], kc)
        n_new = carry_decay[..., None] * n + jnp.einsum('bht,bhtk->bhk', w, kc)
        return (C_new, n_new, m_new), h

    init = (C0.astype(jnp.float32), n0.astype(jnp.float32), m0.astype(jnp.float32))
    (C, n, m), h = lax.scan(step, init, (chunks(q), chunks(k), chunks(v), chunks(ig), chunks(lf)))
    h = jnp.moveaxis(h, 0, 2).reshape(B, H, T, DH)
    return h, C, n, m


def mlstm_mixer(xn, w_up, conv_w, conv_b, w_qkv, w_gate, b_gate, head_g, skip, w_down, init):
    B, T, _ = xn.shape
    xm, z = jnp.split(xn @ w_up, 2, axis=-1)
    xc = jax.nn.silu(centred_dwconv(xm, conv_w, conv_b))
    q = headwise(xc, w_qkv[0])
    k = headwise(xc, w_qkv[1])
    v = headwise(xm, w_qkv[2])
    g = jnp.einsum('btc,dcg->dbtg', jnp.concatenate([q, k, v], axis=-1), w_gate) + b_gate[:, None, None, :]
    g = jnp.moveaxis(g.astype(jnp.float32), -1, 2)
    ig = g[:, :, :ML_NH]
    lf = jax.nn.log_sigmoid(g[:, :, ML_NH:])
    qh = to_heads(q) * ML_DH ** -0.5
    kh = to_heads(k)
    vh = to_heads(v)
    C0, n0, m0 = init
    h_f, Cf, nf, mf = mlstm_scan(qh, kh, vh, ig[0], lf[0], C0[:, 0], n0[:, 0], m0[:, 0])
    flip = lambda a: jnp.flip(a, axis=2)
    h_b, Cb, nb, mb = mlstm_scan(flip(qh), flip(kh), flip(vh), flip(ig[1]), flip(lf[1]), C0[:, 1], n0[:, 1], m0[:, 1])
    h = (h_f + flip(h_b)).transpose(0, 2, 1, 3)
    h = rms_norm(h, head_g.reshape(ML_NH, ML_DH)).reshape(B, T, ML_DI).astype(xn.dtype)
    out = ((h + skip * xc) * jax.nn.silu(z)) @ w_down
    return out, (jnp.stack([Cf, Cb], axis=1), jnp.stack([nf, nb], axis=1), jnp.stack([mf, mb], axis=1))


def setup_inputs(seed: int = 0) -> dict:
    key = jax.random.key(seed)
    ks = iter(jax.random.split(key, 40))

    def nrm(shape, scale):
        return jax.random.normal(next(ks), shape, jnp.float32) * scale

    def gain(shape):
        return 1.0 + nrm(shape, 0.02)

    b_gate = jnp.concatenate([nrm((N_ML, 2, ML_NH), 0.1),
                              jnp.linspace(3.0, 6.0, ML_NH, dtype=jnp.float32) + nrm((N_ML, 2, ML_NH), 0.1)], axis=-1)
    return {
        'x_prompt': nrm((BATCH, SEQ, D_MODEL), 1.0),
        'x_sample': nrm((DEC_BATCH, DEC_SEQ, D_MODEL), 1.0),
        'state_mlstm_C': nrm((DEC_BATCH, N_ML, 2, ML_NH, ML_DH, ML_DH), 0.05),
        'state_mlstm_n': nrm((DEC_BATCH, N_ML, 2, ML_NH, ML_DH), 0.1),
        'state_mlstm_m': nrm((DEC_BATCH, N_ML, 2, ML_NH), 1.0),
        'cache_mla': nrm((DEC_BATCH, N_MLA, PAST_LEN, MLA_KV_LORA + MLA_ROPE), 1.0),
        'c': nrm((DEC_BATCH, D_MODEL), 1.0),
        'c_ctx': nrm((D_MODEL,), 1.0),
        'w_mod': nrm((DEPTH, D_MODEL, 9 * D_MODEL), 0.5 * D_MODEL ** -0.5),
        'b_mod': nrm((DEPTH, 9 * D_MODEL), 0.02),
        'norm_g': gain((DEPTH, 3, D_MODEL)),
        'ffn_w_in': nrm((DEPTH, 2, D_MODEL, 2 * D_FF), D_MODEL ** -0.5),
        'ffn_w_out': nrm((DEPTH, 2, D_FF, D_MODEL), D_FF ** -0.5),
        'ml_w_up': nrm((N_ML, D_MODEL, 2 * ML_DI), D_MODEL ** -0.5),
        'ml_conv_w': nrm((N_ML, ML_CONV_K, ML_DI), ML_CONV_K ** -0.5),
        'ml_conv_b': nrm((N_ML, ML_DI), 0.01),
        'ml_w_qkv': nrm((N_ML, 3, ML_DI // ML_QKV_BLOCK, ML_QKV_BLOCK, ML_QKV_BLOCK), ML_QKV_BLOCK ** -0.5),
        'ml_w_gate': nrm((N_ML, 2, 3 * ML_DI, 2 * ML_NH), (3 * ML_DI) ** -0.5),
        'ml_b_gate': b_gate,
        'ml_head_g': gain((N_ML, ML_DI)),
        'ml_skip': gain((N_ML, ML_DI)),
        'ml_w_down': nrm((N_ML, ML_DI, D_MODEL), ML_DI ** -0.5),
        'mla_w_in': nrm((N_MLA, D_MODEL, MLA_Q_LORA + MLA_KV_LORA + MLA_ROPE), D_MODEL ** -0.5),
        'mla_q_norm_g': gain((N_MLA, MLA_Q_LORA)),
        'mla_w_uq': nrm((N_MLA, MLA_Q_LORA, MLA_NH * MLA_QK), MLA_Q_LORA ** -0.5),
        'mla_kv_norm_g': gain((N_MLA, MLA_KV_LORA)),
        'mla_w_ukv': nrm((N_MLA, MLA_KV_LORA, MLA_NH * (MLA_NOPE + MLA_DV)), MLA_KV_LORA ** -0.5),
        'mla_qk_g': gain((N_MLA, 2, MLA_QK)),
        'mla_w_o': nrm((N_MLA, MLA_NH * MLA_DV, D_MODEL), (MLA_NH * MLA_DV) ** -0.5),
    }


def reference(x_prompt, x_sample, state_mlstm_C, state_mlstm_n, state_mlstm_m, cache_mla, c, c_ctx,
              w_mod, b_mod, norm_g, ffn_w_in, ffn_w_out,
              ml_w_up, ml_conv_w, ml_conv_b, ml_w_qkv, ml_w_gate, ml_b_gate, ml_head_g, ml_skip, ml_w_down,
              mla_w_in, mla_q_norm_g, mla_w_uq, mla_kv_norm_g, mla_w_ukv, mla_qk_g, mla_w_o):
    rope = grid_rope(x_sample.shape[1])
    b_ctx = x_prompt.shape[0]
    zero_state = (jnp.zeros((b_ctx, 2, ML_NH, ML_DH, ML_DH), jnp.float32),
                  jnp.zeros((b_ctx, 2, ML_NH, ML_DH), jnp.float32),
                  jnp.zeros((b_ctx, 2, ML_NH), jnp.float32))
    yp, ys = x_prompt, x_sample
    new_C, new_n, new_m, new_kv = [], [], [], []
    for i in range(DEPTH):
        j = i // N_MIXERS
        mod_ctx = (jax.nn.silu(c_ctx) @ w_mod[i] + b_mod[i])[None, None, :]
        mod_lat = (jax.nn.silu(c) @ w_mod[i] + b_mod[i])[:, None, :]
        if i % N_MIXERS == 0:
            ml_args = (ml_w_up[j], ml_conv_w[j], ml_conv_b[j], ml_w_qkv[j], ml_w_gate[j], ml_b_gate[j],
                       ml_head_g[j], ml_skip[j], ml_w_down[j])
            lat_init = (state_mlstm_C[:, j], state_mlstm_n[:, j], state_mlstm_m[:, j])
            yp, st = trunk_layer(yp, mod_ctx, norm_g[i], ffn_w_in[i], ffn_w_out[i],
                                 lambda xn: mlstm_mixer(xn, *ml_args, zero_state))
            ys, _ = trunk_layer(ys, mod_lat, norm_g[i], ffn_w_in[i], ffn_w_out[i],
                                lambda xn: mlstm_mixer(xn, *ml_args, lat_init))
            new_C.append(st[0])
            new_n.append(st[1])
            new_m.append(st[2])
        else:
            mla_args = (mla_w_in[j], mla_q_norm_g[j], mla_w_uq[j], mla_kv_norm_g[j], mla_w_ukv[j], mla_qk_g[j], mla_w_o[j])
            yp, entry = trunk_layer(yp, mod_ctx, norm_g[i], ffn_w_in[i], ffn_w_out[i],
                                    lambda xn: mla_mixer(xn, *mla_args, None, None))
            ys, _ = trunk_layer(ys, mod_lat, norm_g[i], ffn_w_in[i], ffn_w_out[i],
                                lambda xn: mla_mixer(xn, *mla_args, cache_mla[:, j], rope))
            new_kv.append(entry)
    return (yp, ys, jnp.stack(new_C, axis=1), jnp.stack(new_n, axis=1), jnp.stack(new_m, axis=1), jnp.stack(new_kv, axis=1))
```

```python
import functools

import jax
import jax.numpy as jnp
import numpy as np
from jax import lax
from jax.experimental import pallas as pl
from jax.experimental.pallas import tpu as pltpu

F32 = jnp.float32
BF16 = jnp.bfloat16

D_MODEL = 1024
BATCH = 32
SEQ = 256
DEPTH = 4
DEC_BATCH = 2
DEC_SEQ = 4096
PAST_LEN = 512
GRID_W = 64
N_ML = 2
N_MLA = 2
D_FF = 2816
EPS = 1e-6
ML_DI = 2048
ML_NH = 4
ML_DH = 512
ML_CONV_K = 5
ML_QKV_BLOCK = 4
MLA_NH = 8
MLA_NOPE = 128
MLA_ROPE = 64
MLA_DV = 128
MLA_QK = 192
MLA_Q_LORA = 256
MLA_KV_LORA = 128
ROPE_THETA = 10000.0

NP_TOK = BATCH * SEQ
NS_TOK = DEC_BATCH * DEC_SEQ
T_TOK = NP_TOK + NS_TOK
N_MODVEC = 9
MOD_ROWS = 8

SUBLANES = 8
LANES = 128
MXU_DIM = 256
VMEM_LIMIT = 56 * 1024 * 1024

CHUNK = 256
HEAD_SLOT = 2 * LANES
HALO = SUBLANES

TM_FFN = 512
TM_OUT = 512
TM_MLA = 512
TQ_ATT = 512
TN_MOD = 2304


def _cparams(*sem):
    return pltpu.CompilerParams(dimension_semantics=tuple(sem), vmem_limit_bytes=VMEM_LIMIT)


def _const_spec(shape):
    nd = len(shape)
    return pl.BlockSpec(shape, lambda *_: (0,) * nd, pipeline_mode=pl.Buffered(1))


def _mod_row(tok0):
    return jnp.where(tok0 < NP_TOK, 0, 1 + (tok0 - NP_TOK) // DEC_SEQ)


def _mod_spec(tm):
    return pl.BlockSpec((None, N_MODVEC, D_MODEL), lambda i: (_mod_row(i * tm), 0, 0))


def _silu(x):
    return x * jax.nn.sigmoid(x)


def _modulate(x, g, sh, sc):
    ms = jnp.mean(x * x, axis=-1, keepdims=True)
    return (x * lax.rsqrt(ms + EPS) * g) * (1.0 + sc) + sh


def _dot(a, b):
    return jnp.dot(a, b, preferred_element_type=F32)


def _dot_nt(a, b):
    return lax.dot_general(a, b, (((1,), (1,)), ((), ())), preferred_element_type=F32)


def _dot_tn(a, b):
    return lax.dot_general(a, b, (((0,), (0,)), ((), ())), preferred_element_type=F32)


def _mod_kernel(c_ref, w_ref, b_ref, o_ref):
    s = _silu(c_ref[...]).astype(BF16)
    o_ref[...] = _dot(s, w_ref[...].astype(BF16)) + b_ref[...]


def _mod_call(cvecs, w_mod, b_mod):
    nt = (N_MODVEC * D_MODEL) // TN_MOD
    out = pl.pallas_call(
        _mod_kernel,
        grid=(DEPTH, nt),
        in_specs=[
            pl.BlockSpec((MOD_ROWS, D_MODEL), lambda l, j: (0, 0)),
            pl.BlockSpec((None, D_MODEL, TN_MOD), lambda l, j: (l, 0, j)),
            pl.BlockSpec((None, 1, TN_MOD), lambda l, j: (l, 0, j)),
        ],
        out_specs=pl.BlockSpec((None, MOD_ROWS, TN_MOD), lambda l, j: (l, 0, j)),
        out_shape=jax.ShapeDtypeStruct((DEPTH, MOD_ROWS, N_MODVEC * D_MODEL), F32),
        compiler_params=_cparams("parallel", "parallel"),
        name="mod",
    )(cvecs, w_mod, b_mod.reshape(DEPTH, 1, N_MODVEC * D_MODEL))
    return out.reshape(DEPTH, MOD_ROWS, N_MODVEC, D_MODEL)


def _ffn_kernel(x_ref, mod_ref, g_ref, win_ref, wout_ref, o_ref, *, sub):
    x = x_ref[...]
    sh = mod_ref[3 * sub:3 * sub + 1, :]
    sc = mod_ref[3 * sub + 1:3 * sub + 2, :]
    ga = mod_ref[3 * sub + 2:3 * sub + 3, :]
    xn = _modulate(x, g_ref[sub:sub + 1, :], sh, sc).astype(BF16)
    ab = _dot(xn, win_ref[...])
    h = (_silu(ab[:, :D_FF]) * ab[:, D_FF:]).astype(BF16)
    o_ref[...] = x + (0.5 * ga) * _dot(h, wout_ref[...])


def _ffn_call(x, mod_l, g3, w_in, w_out, sub):
    tm = TM_FFN
    return pl.pallas_call(
        functools.partial(_ffn_kernel, sub=sub),
        grid=(T_TOK // tm,),
        in_specs=[
            pl.BlockSpec((tm, D_MODEL), lambda i: (i, 0)),
            _mod_spec(tm),
            _const_spec((3, D_MODEL)),
            _const_spec((D_MODEL, 2 * D_FF)),
            _const_spec((D_FF, D_MODEL)),
        ],
        out_specs=pl.BlockSpec((tm, D_MODEL), lambda i: (i, 0)),
        out_shape=jax.ShapeDtypeStruct((T_TOK, D_MODEL), F32),
        compiler_params=_cparams("parallel"),
        name="ffn",
    )(x, mod_l, g3, w_in, w_out)


def _ml_pre_kernel(x_ref, xp_ref, xn_ref, mod_ref, g_ref, wup_ref, cw_ref, cb_ref, wq_ref, wk_ref, wv_ref,
                   wg_ref, bg_ref, q_ref, k_ref, v_ref, xc_ref, z_ref, gp_ref, xm_scr):
    i = pl.program_id(0)
    tm = CHUNK
    n_ptile = NP_TOK // tm
    per_seq = DEC_SEQ // tm
    pos = (i - n_ptile) % per_seq
    has_prev = jnp.logical_and(i >= n_ptile, pos != 0)
    has_next = jnp.logical_and(i >= n_ptile, pos != per_seq - 1)

    xe = jnp.concatenate([xp_ref[...], x_ref[...], xn_ref[...]], axis=0)
    xn = _modulate(xe, g_ref[1:2, :], mod_ref[3:4, :], mod_ref[4:5, :]).astype(BF16)
    up = _dot(xn, wup_ref[...])
    row = lax.broadcasted_iota(jnp.int32, (tm + 2 * HALO, 1), 0)
    valid = jnp.logical_and(jnp.logical_or(row >= HALO, has_prev),
                            jnp.logical_or(row < tm + HALO, has_next))
    xm_scr[...] = jnp.where(valid, up[:, :ML_DI], 0.0)
    z_ref[...] = up[HALO:HALO + tm, ML_DI:].astype(BF16)

    conv = cb_ref[...] + cw_ref[0:1, :] * xm_scr[pl.ds(HALO - 2, tm), :]
    for j in range(1, ML_CONV_K):
        conv = conv + cw_ref[j:j + 1, :] * xm_scr[pl.ds(HALO - 2 + j, tm), :]
    xc = _silu(conv)
    xc_ref[...] = xc.astype(BF16)
    xc_b = xc.astype(BF16)
    xm_b = xm_scr[pl.ds(HALO, tm), :].astype(BF16)

    nblk = ML_DI // MXU_DIM
    def headwise(src, w_ref):
        return jnp.concatenate(
            [_dot(src[:, b * MXU_DIM:(b + 1) * MXU_DIM], w_ref[b]) for b in range(nblk)], axis=-1)
    q = headwise(xc_b, wq_ref)
    k = headwise(xc_b, wk_ref)
    v = headwise(xm_b, wv_ref)
    qb = q.astype(BF16)
    kb = k.astype(BF16)
    vb = v.astype(BF16)
    q_ref[...] = (q * (ML_DH ** -0.5)).astype(BF16)
    k_ref[...] = kb
    v_ref[...] = vb

    g = (_dot(qb, wg_ref[0:ML_DI, :]) + _dot(kb, wg_ref[ML_DI:2 * ML_DI, :])
         + _dot(vb, wg_ref[2 * ML_DI:, :]) + bg_ref[...])
    lf = jnp.minimum(g, 0.0) - jnp.log1p(jnp.exp(-jnp.abs(g)))
    jj = lax.broadcasted_iota(jnp.int32, (tm, tm), 0)
    tt = lax.broadcasted_iota(jnp.int32, (tm, tm), 1)
    lower = (tt <= jj).astype(F32)
    upper = (tt >= jj).astype(F32)
    cum_f = jnp.dot(lower, lf, precision=lax.Precision.HIGHEST, preferred_element_type=F32)
    cum_b = jnp.dot(upper, lf, precision=lax.Precision.HIGHEST, preferred_element_type=F32)
    lane = lax.broadcasted_iota(jnp.int32, g.shape, 1)
    gp_ref[...] = jnp.where(lane % (2 * ML_NH) < ML_NH, g, jnp.where(lane < 2 * ML_NH, cum_f, cum_b))


def _ml_pre_call(x, mod_l, g3, w_up, conv_w, conv_b, wq, wk, wv, wg, bg):
    tm = CHUNK
    r = tm // HALO
    nblk = ML_DI // MXU_DIM
    tok = lambda i: (i, 0)
    act = jax.ShapeDtypeStruct((T_TOK, ML_DI), BF16)
    return pl.pallas_call(
        _ml_pre_kernel,
        grid=(T_TOK // tm,),
        in_specs=[
            pl.BlockSpec((tm, D_MODEL), tok),
            pl.BlockSpec((HALO, D_MODEL), lambda i: (jnp.maximum(i * r - 1, 0), 0)),
            pl.BlockSpec((HALO, D_MODEL), lambda i: (jnp.minimum((i + 1) * r, T_TOK // HALO - 1), 0)),
            _mod_spec(tm),
            _const_spec((3, D_MODEL)),
            _const_spec((D_MODEL, 2 * ML_DI)),
            _const_spec((ML_CONV_K, ML_DI)),
            _const_spec((1, ML_DI)),
            _const_spec((nblk, MXU_DIM, MXU_DIM)),
            _const_spec((nblk, MXU_DIM, MXU_DIM)),
            _const_spec((nblk, MXU_DIM, MXU_DIM)),
            _const_spec((3 * ML_DI, 4 * ML_NH)),
            _const_spec((1, 4 * ML_NH)),
        ],
        out_specs=[pl.BlockSpec((tm, ML_DI), tok)] * 5 + [pl.BlockSpec((tm, 4 * ML_NH), tok)],
        out_shape=[act] * 5 + [jax.ShapeDtypeStruct((T_TOK, 4 * ML_NH), F32)],
        scratch_shapes=[pltpu.VMEM((tm + 2 * HALO, ML_DI), F32)],
        compiler_params=_cparams("parallel"),
        name="ml_pre",
    )(x, x, x, mod_l, g3, w_up, conv_w, conv_b, wq, wk, wv, wg, bg)


def _mlstm_dir(s_qk, q, k, v, gc, gr, col0, mask, b_last, state):
    ig_c = gc[:, col0:col0 + 1]
    cum_c = gc[:, col0 + 1:col0 + 2]
    ig_r = gr[col0:col0 + 1, :]
    cum_r = gr[col0 + 1:col0 + 2, :]
    dlog = jnp.where(mask, (cum_c - cum_r) + ig_r, -jnp.inf)
    if state is None:
        m0 = 0.0
    else:
        c0, n0, m0 = state
    inter = cum_c + m0
    m_row = jnp.maximum(jnp.max(dlog, axis=1, keepdims=True), inter)
    sd = s_qk * jnp.exp(dlog - m_row)
    num = _dot(sd.astype(BF16), v)
    den = jnp.sum(sd, axis=1, keepdims=True)
    if state is not None:
        decay = jnp.exp(inter - m_row)
        num = decay * _dot_nt(q, c0.astype(BF16)) + num
        den = decay * jnp.sum(q.astype(F32) * n0, axis=1, keepdims=True) + den
    h = num / jnp.maximum(jnp.abs(den), jnp.exp(-m_row))
    wlog_c = (b_last - cum_c) + ig_c
    wlog_r = (b_last - cum_r) + ig_r
    m_new = jnp.maximum(b_last + m0, jnp.max(wlog_r, axis=1, keepdims=True))
    w_c = jnp.exp(wlog_c - m_new)
    w_r = jnp.exp(wlog_r - m_new)
    c_new = _dot_tn((v.astype(F32) * w_c).astype(BF16), k)
    n_new = _dot(jnp.broadcast_to(w_r, (SUBLANES, w_r.shape[1])).astype(BF16), k)[0:1, :]
    if state is not None:
        carry = jnp.exp(b_last + m0 - m_new)
        c_new = carry * c0 + c_new
        n_new = carry * n0 + n_new
    return h, c_new, n_new, m_new


def _chunk_masks():
    jj = lax.broadcasted_iota(jnp.int32, (CHUNK, CHUNK), 0)
    tt = lax.broadcasted_iota(jnp.int32, (CHUNK, CHUNK), 1)
    return tt <= jj, tt >= jj


def _scan_prompt_kernel(q_ref, k_ref, v_ref, gc_ref, gr_ref, h_ref, c_ref, n_ref, m_ref):
    q = q_ref[...]
    k = k_ref[...]
    v = v_ref[...]
    gc = gc_ref[...]
    gr = gr_ref[...]
    s_qk = _dot_nt(q, k)
    causal, anti = _chunk_masks()
    hf, cf, nf, mf = _mlstm_dir(s_qk, q, k, v, gc, gr, 0, causal, gr[1:2, CHUNK - 1:CHUNK], None)
    hb, cb, nb, mb = _mlstm_dir(s_qk, q, k, v, gc, gr, 2, anti, gr[3:4, 0:1], None)
    h_ref[...] = hf + hb
    c_ref[0] = cf
    c_ref[1] = cb
    n_ref[0] = nf
    n_ref[1] = nb
    m_ref[0] = jnp.broadcast_to(mf, (1, LANES))
    m_ref[1] = jnp.broadcast_to(mb, (1, LANES))


def _scan_prompt_call(q, k, v, gpc, gpr):
    blk = lambda b, h: (b, h)
    return pl.pallas_call(
        _scan_prompt_kernel,
        grid=(BATCH, ML_NH),
        in_specs=[
            pl.BlockSpec((CHUNK, ML_DH), blk),
            pl.BlockSpec((CHUNK, ML_DH), blk),
            pl.BlockSpec((CHUNK, ML_DH), blk),
            pl.BlockSpec((None, CHUNK, 4), lambda b, h: (h, b, 0)),
            pl.BlockSpec((None, 4, CHUNK), lambda b, h: (h, 0, b)),
        ],
        out_specs=[
            pl.BlockSpec((CHUNK, ML_DH), blk),
            pl.BlockSpec((None, 2, None, ML_DH, ML_DH), lambda b, h: (b, 0, h, 0, 0)),
            pl.BlockSpec((None, 2, None, 1, ML_DH), lambda b, h: (b, 0, h, 0, 0)),
            pl.BlockSpec((None, 2, None, 1, LANES), lambda b, h: (b, 0, h, 0, 0)),
        ],
        out_shape=[
            jax.ShapeDtypeStruct((NP_TOK, ML_DI), F32),
            jax.ShapeDtypeStruct((BATCH, 2, ML_NH, ML_DH, ML_DH), F32),
            jax.ShapeDtypeStruct((BATCH, 2, ML_NH, 1, ML_DH), F32),
            jax.ShapeDtypeStruct((BATCH, 2, ML_NH, 1, LANES), F32),
        ],
        compiler_params=_cparams("parallel", "parallel"),
        name="scan_prompt",
    )(q, k, v, gpc, gpr)


def _scan_sample_kernel(qf_ref, kf_ref, vf_ref, gcf_ref, grf_ref, qb_ref, kb_ref, vb_ref, gcb_ref, grb_ref,
                        c0_ref, n0_ref, m0_ref, hf_ref, hb_ref, c_scr, n_scr, m_scr):
    @pl.when(pl.program_id(1) == 0)
    def _():
        c_scr[...] = c0_ref[...]
        n_scr[...] = n0_ref[...]
        m_scr[...] = m0_ref[...]

    causal, anti = _chunk_masks()

    q = qf_ref[...]
    k = kf_ref[...]
    gr = grf_ref[...]
    h, c_new, n_new, m_new = _mlstm_dir(_dot_nt(q, k), q, k, vf_ref[...], gcf_ref[...], gr, 0, causal,
                                        gr[1:2, CHUNK - 1:CHUNK], (c_scr[0], n_scr[0], m_scr[0][:, 0:1]))
    hf_ref[...] = h
    c_scr[0] = c_new
    n_scr[0] = n_new
    m_scr[0] = jnp.broadcast_to(m_new, (1, LANES))

    q = qb_ref[...]
    k = kb_ref[...]
    gr = grb_ref[...]
    h, c_new, n_new, m_new = _mlstm_dir(_dot_nt(q, k), q, k, vb_ref[...], gcb_ref[...], gr, 2, anti,
                                        gr[3:4, 0:1], (c_scr[1], n_scr[1], m_scr[1][:, 0:1]))
    hb_ref[...] = h
    c_scr[1] = c_new
    n_scr[1] = n_new
    m_scr[1] = jnp.broadcast_to(m_new, (1, LANES))


def _scan_sample_call(q, k, v, gpc, gpr, c0, n0, m0, layer):
    nch = DEC_SEQ // CHUNK
    p0 = NP_TOK // CHUNK
    fwd = lambda bh, s: (p0 + (bh // ML_NH) * nch + s, bh % ML_NH)
    bwd = lambda bh, s: (p0 + (bh // ML_NH) * nch + (nch - 1 - s), bh % ML_NH)
    gcf = lambda bh, s: (bh % ML_NH, p0 + (bh // ML_NH) * nch + s, 0)
    grf = lambda bh, s: (bh % ML_NH, 0, p0 + (bh // ML_NH) * nch + s)
    gcb = lambda bh, s: (bh % ML_NH, p0 + (bh // ML_NH) * nch + (nch - 1 - s), 0)
    grb = lambda bh, s: (bh % ML_NH, 0, p0 + (bh // ML_NH) * nch + (nch - 1 - s))
    st = lambda bh, s: (bh // ML_NH, layer, 0, bh % ML_NH, 0, 0)
    ofwd = lambda bh, s: ((bh // ML_NH) * nch + s, bh % ML_NH)
    obwd = lambda bh, s: ((bh // ML_NH) * nch + (nch - 1 - s), bh % ML_NH)
    qkv = lambda im: pl.BlockSpec((CHUNK, ML_DH), im)
    return pl.pallas_call(
        _scan_sample_kernel,
        grid=(DEC_BATCH * ML_NH, nch),
        in_specs=[
            qkv(fwd), qkv(fwd), qkv(fwd),
            pl.BlockSpec((None, CHUNK, 4), gcf), pl.BlockSpec((None, 4, CHUNK), grf),
            qkv(bwd), qkv(bwd), qkv(bwd),
            pl.BlockSpec((None, CHUNK, 4), gcb), pl.BlockSpec((None, 4, CHUNK), grb),
            pl.BlockSpec((None, None, 2, None, ML_DH, ML_DH), st),
            pl.BlockSpec((None, None, 2, None, 1, ML_DH), st),
            pl.BlockSpec((None, None, 2, None, 1, LANES), st),
        ],
        out_specs=[pl.BlockSpec((CHUNK, ML_DH), ofwd), pl.BlockSpec((CHUNK, ML_DH), obwd)],
        out_shape=[jax.ShapeDtypeStruct((NS_TOK, ML_DI), F32)] * 2,
        scratch_shapes=[
            pltpu.VMEM((2, ML_DH, ML_DH), F32),
            pltpu.VMEM((2, 1, ML_DH), F32),
            pltpu.VMEM((2, 1, LANES), F32),
        ],
        compiler_params=_cparams("parallel", "arbitrary"),
        name="scan_sample",
    )(q, k, v, gpc, gpr, q, k, v, gpc, gpr, c0, n0, m0)


def _ml_out_kernel(x_ref, hp_ref, hf_ref, hb_ref, xc_ref, z_ref, mod_ref, hg_ref, skip_ref, wd_ref, o_ref, h_scr,
                   *, n_ptile):
    i = pl.program_id(0)

    @pl.when(i < n_ptile)
    def _():
        h_scr[...] = hp_ref[...]

    @pl.when(i >= n_ptile)
    def _():
        h_scr[...] = hf_ref[...] + hb_ref[...]

    parts = []
    for hd in range(ML_NH):
        sl = slice(hd * ML_DH, (hd + 1) * ML_DH)
        hh = h_scr[:, sl]
        ms = jnp.mean(hh * hh, axis=-1, keepdims=True)
        hn = hh * lax.rsqrt(ms + EPS) * hg_ref[:, sl]
        u = (hn + skip_ref[:, sl] * xc_ref[:, sl].astype(F32)) * _silu(z_ref[:, sl].astype(F32))
        parts.append(u.astype(BF16))
    y = _dot(jnp.concatenate(parts, axis=-1), wd_ref[...])
    o_ref[...] = x_ref[...] + mod_ref[5:6, :] * y


def _ml_out_call(x, hp, hf, hb, xc, z, mod_l, head_g, skip, w_down):
    tm = TM_OUT
    n_ptile = NP_TOK // tm
    tok = lambda i: (i, 0)
    return pl.pallas_call(
        functools.partial(_ml_out_kernel, n_ptile=n_ptile),
        grid=(T_TOK // tm,),
        in_specs=[
            pl.BlockSpec((tm, D_MODEL), tok),
            pl.BlockSpec((tm, ML_DI), lambda i: (jnp.minimum(i, n_ptile - 1), 0)),
            pl.BlockSpec((tm, ML_DI), lambda i: (jnp.maximum(i - n_ptile, 0), 0)),
            pl.BlockSpec((tm, ML_DI), lambda i: (jnp.maximum(i - n_ptile, 0), 0)),
            pl.BlockSpec((tm, ML_DI), tok),
            pl.BlockSpec((tm, ML_DI), tok),
            _mod_spec(tm),
            _const_spec((1, ML_DI)),
            _const_spec((1, ML_DI)),
            _const_spec((ML_DI, D_MODEL)),
        ],
        out_specs=pl.BlockSpec((tm, D_MODEL), tok),
        out_shape=jax.ShapeDtypeStruct((T_TOK, D_MODEL), F32),
        scratch_shapes=[pltpu.VMEM((tm, ML_DI), F32)],
        compiler_params=_cparams("arbitrary"),
        name="ml_out",
    )(x, hp, hf, hb, xc, z, mod_l, head_g, skip, w_down)


def _rope(x, cs, sn):
    return x * cs + pltpu.roll(x, LANES // 2, 1) * sn


def _mla_k_heads(kn, krp, kg_ref, cs, sn, k_ref):
    ss_r = jnp.sum(krp * krp, axis=-1, keepdims=True)
    for hd in range(MLA_NH):
        kk = kn[:, hd * LANES:(hd + 1) * LANES]
        ss = jnp.sum(kk * kk, axis=-1, keepdims=True) + ss_r
        rinv = lax.rsqrt(ss * (1.0 / MLA_QK) + EPS)
        k_ref[:, hd * HEAD_SLOT:hd * HEAD_SLOT + LANES] = (kk * rinv * kg_ref[:, 0:LANES]).astype(BF16)
        kr = krp * rinv * kg_ref[:, LANES:]
        if cs is not None:
            kr = _rope(kr, cs, sn)
        k_ref[:, hd * HEAD_SLOT + LANES:(hd + 1) * HEAD_SLOT] = kr.astype(BF16)


def _mla_pre_kernel(x_ref, mod_ref, g_ref, win_ref, qng_ref, wuq_ref, kvng_ref, wuk_ref, wuv_ref, qg_ref, kg_ref,
                    cs_ref, sn_ref, q_ref, k_ref, v_ref, e_ref):
    xn = _modulate(x_ref[...], g_ref[1:2, :], mod_ref[3:4, :], mod_ref[4:5, :]).astype(BF16)
    h = _dot(xn, win_ref[...])
    cq = h[:, 0:MLA_Q_LORA]
    ckv = h[:, MLA_Q_LORA:MLA_Q_LORA + MLA_KV_LORA]
    krp = h[:, MLA_Q_LORA + MLA_KV_LORA:MLA_Q_LORA + MLA_KV_LORA + LANES]
    kr = h[:, MLA_Q_LORA + MLA_KV_LORA + LANES:]
    cs = cs_ref[...]
    sn = sn_ref[...]

    cqn = cq * lax.rsqrt(jnp.mean(cq * cq, axis=-1, keepdims=True) + EPS) * qng_ref[...]
    q = _dot(cqn.astype(BF16), wuq_ref[...])
    scale = MLA_QK ** -0.5
    for hd in range(MLA_NH):
        qs = q[:, hd * HEAD_SLOT:(hd + 1) * HEAD_SLOT]
        rinv = lax.rsqrt(jnp.sum(qs * qs, axis=-1, keepdims=True) * (1.0 / MLA_QK) + EPS) * scale
        q_ref[:, hd * HEAD_SLOT:hd * HEAD_SLOT + LANES] = (qs[:, :LANES] * rinv * qg_ref[:, 0:LANES]).astype(BF16)
        qr = _rope(qs[:, LANES:] * rinv * qg_ref[:, LANES:], cs, sn)
        q_ref[:, hd * HEAD_SLOT + LANES:(hd + 1) * HEAD_SLOT] = qr.astype(BF16)

    ckvn = ckv * lax.rsqrt(jnp.mean(ckv * ckv, axis=-1, keepdims=True) + EPS) * kvng_ref[...]
    e_ref[:, 0:MLA_KV_LORA] = ckvn
    e_ref[:, MLA_KV_LORA:] = kr
    cb = ckvn.astype(BF16)
    v_ref[...] = _dot(cb, wuv_ref[...]).astype(BF16)
    _mla_k_heads(_dot(cb, wuk_ref[...]), krp, kg_ref, cs, sn, k_ref)


def _mla_pre_call(x, mod_l, g3, w_in_ext, qng, wuq, kvng, wuk, wuv, qg, kg, cs_tab, sn_tab):
    tm = TM_MLA
    n_ptile = NP_TOK // tm
    per_seq = DEC_SEQ // tm
    tok = lambda i: (i, 0)
    rot = lambda i: (jnp.where(i < n_ptile, 0, 1 + (i - n_ptile) % per_seq), 0)
    n_in = w_in_ext.shape[1]
    return pl.pallas_call(
        _mla_pre_kernel,
        grid=(T_TOK // tm,),
        in_specs=[
            pl.BlockSpec((tm, D_MODEL), tok),
            _mod_spec(tm),
            _const_spec((3, D_MODEL)),
            _const_spec((D_MODEL, n_in)),
            _const_spec((1, MLA_Q_LORA)),
            _const_spec((MLA_Q_LORA, MLA_NH * HEAD_SLOT)),
            _const_spec((1, MLA_KV_LORA)),
            _const_spec((MLA_KV_LORA, MLA_NH * MLA_NOPE)),
            _const_spec((MLA_KV_LORA, MLA_NH * MLA_DV)),
            _const_spec((1, HEAD_SLOT)),
            _const_spec((1, HEAD_SLOT)),
            pl.BlockSpec((tm, LANES), rot),
            pl.BlockSpec((tm, LANES), rot),
        ],
        out_specs=[
            pl.BlockSpec((tm, MLA_NH * HEAD_SLOT), tok),
            pl.BlockSpec((tm, MLA_NH * HEAD_SLOT), tok),
            pl.BlockSpec((tm, MLA_NH * MLA_DV), tok),
            pl.BlockSpec((tm, MLA_KV_LORA + MLA_ROPE), tok),
        ],
        out_shape=[
            jax.ShapeDtypeStruct((T_TOK, MLA_NH * HEAD_SLOT), BF16),
            jax.ShapeDtypeStruct((T_TOK, MLA_NH * HEAD_SLOT), BF16),
            jax.ShapeDtypeStruct((T_TOK, MLA_NH * MLA_DV), BF16),
            jax.ShapeDtypeStruct((T_TOK, MLA_KV_LORA + MLA_ROPE), F32),
        ],
        compiler_params=_cparams("parallel"),
        name="mla_pre",
    )(x, mod_l, g3, w_in_ext, qng, wuq, kvng, wuk, wuv, qg, kg, cs_tab, sn_tab)


def _mla_ctx_kernel(ckvn_ref, krp_ref, wuk_ref, wuv_ref, kg_ref, k_ref, v_ref):
    cb = ckvn_ref[...].astype(BF16)
    v_ref[...] = _dot(cb, wuv_ref[...]).astype(BF16)
    _mla_k_heads(_dot(cb, wuk_ref[...]), krp_ref[...], kg_ref, None, None, k_ref)


def _mla_ctx_call(ckvn, krp, wuk, wuv, kg):
    rows = ckvn.shape[0]
    tm = PAST_LEN
    tok = lambda i: (i, 0)
    return pl.pallas_call(
        _mla_ctx_kernel,
        grid=(rows // tm,),
        in_specs=[
            pl.BlockSpec((tm, MLA_KV_LORA), tok),
            pl.BlockSpec((tm, LANES), tok),
            _const_spec((MLA_KV_LORA, MLA_NH * MLA_NOPE)),
            _const_spec((MLA_KV_LORA, MLA_NH * MLA_DV)),
            _const_spec((1, HEAD_SLOT)),
        ],
        out_specs=[pl.BlockSpec((tm, MLA_NH * HEAD_SLOT), tok), pl.BlockSpec((tm, MLA_NH * MLA_DV), tok)],
        out_shape=[jax.ShapeDtypeStruct((rows, MLA_NH * HEAD_SLOT), BF16),
                   jax.ShapeDtypeStruct((rows, MLA_NH * MLA_DV), BF16)],
        compiler_params=_cparams("parallel"),
        name="mla_ctx",
    )(ckvn, krp, wuk, wuv, kg)


def _softmax_pv(q, k, v):
    s = _dot_nt(q, k)
    p = jnp.exp(s - jnp.max(s, axis=-1, keepdims=True))
    l = jnp.sum(p, axis=-1, keepdims=True)
    return _dot(p.astype(BF16), v) / l


def _attn_prompt_kernel(q_ref, k_ref, v_ref, o_ref):
    for hd in range(MLA_NH):
        qs = slice(hd * HEAD_SLOT, (hd + 1) * HEAD_SLOT)
        vs = slice(hd * MLA_DV, (hd + 1) * MLA_DV)
        o_ref[:, vs] = _softmax_pv(q_ref[:, qs], k_ref[:, qs], v_ref[:, vs]).astype(BF16)


def _attn_prompt_call(q, k, v):
    tok = lambda b: (b, 0)
    return pl.pallas_call(
        _attn_prompt_kernel,
        grid=(BATCH,),
        in_specs=[
            pl.BlockSpec((SEQ, MLA_NH * HEAD_SLOT), tok),
            pl.BlockSpec((SEQ, MLA_NH * HEAD_SLOT), tok),
            pl.BlockSpec((SEQ, MLA_NH * MLA_DV), tok),
        ],
        out_specs=pl.BlockSpec((SEQ, MLA_NH * MLA_DV), tok),
        out_shape=jax.ShapeDtypeStruct((NP_TOK, MLA_NH * MLA_DV), BF16),
        compiler_params=_cparams("parallel"),
        name="attn_prompt",
    )(q, k, v)


def _attn_sample_kernel(q_ref, k_ref, v_ref, o_ref):
    o_ref[...] = _softmax_pv(q_ref[...], k_ref[...], v_ref[...]).astype(BF16)


def _attn_sample_call(q, k_all, v_all):
    tq = TQ_ATT
    nq = DEC_SEQ // tq
    p0 = NP_TOK // tq
    tk = DEC_SEQ + PAST_LEN
    return pl.pallas_call(
        _attn_sample_kernel,
        grid=(DEC_BATCH, MLA_NH, nq),
        in_specs=[
            pl.BlockSpec((tq, HEAD_SLOT), lambda b, h, t: (p0 + b * nq + t, h)),
            pl.BlockSpec((None, tk, HEAD_SLOT), lambda b, h, t: (b, 0, h)),
            pl.BlockSpec((None, tk, MLA_DV), lambda b, h, t: (b, 0, h)),
        ],
        out_specs=pl.BlockSpec((tq, MLA_DV), lambda b, h, t: (b * nq + t, h)),
        out_shape=jax.ShapeDtypeStruct((NS_TOK, MLA_NH * MLA_DV), BF16),
        compiler_params=_cparams("parallel", "parallel", "arbitrary"),
        name="attn_sample",
    )(q, k_all, v_all)


def _mla_out_kernel(x_ref, op_ref, os_ref, mod_ref, wo_ref, o_ref, a_scr, *, n_ptile):
    i = pl.program_id(0)

    @pl.when(i < n_ptile)
    def _():
        a_scr[...] = op_ref[...]

    @pl.when(i >= n_ptile)
    def _():
        a_scr[...] = os_ref[...]

    o_ref[...] = x_ref[...] + mod_ref[5:6, :] * _dot(a_scr[...], wo_ref[...])


def _mla_out_call(x, o_p, o_s, mod_l, w_o):
    tm = TM_OUT
    n_ptile = NP_TOK // tm
    tok = lambda i: (i, 0)
    width = MLA_NH * MLA_DV
    return pl.pallas_call(
        functools.partial(_mla_out_kernel, n_ptile=n_ptile),
        grid=(T_TOK // tm,),
        in_specs=[
            pl.BlockSpec((tm, D_MODEL), tok),
            pl.BlockSpec((tm, width), lambda i: (jnp.minimum(i, n_ptile - 1), 0)),
            pl.BlockSpec((tm, width), lambda i: (jnp.maximum(i - n_ptile, 0), 0)),
            _mod_spec(tm),
            _const_spec((width, D_MODEL)),
        ],
        out_specs=pl.BlockSpec((tm, D_MODEL), tok),
        out_shape=jax.ShapeDtypeStruct((T_TOK, D_MODEL), F32),
        scratch_shapes=[pltpu.VMEM((tm, width), BF16)],
        compiler_params=_cparams("arbitrary"),
        name="mla_out",
    )(x, o_p, o_s, mod_l, w_o)


def _rope_pad_layout(a):
    half = MLA_ROPE // 2
    z = jnp.zeros(a.shape[:-1] + (LANES // 2 - half,), a.dtype)
    return jnp.concatenate([a[..., :half], z, a[..., half:], z], axis=-1)


def _rope_tables(tm):
    rows = DEC_SEQ // GRID_W
    r = jnp.repeat(jnp.arange(rows, dtype=F32), GRID_W)
    col = jnp.tile(jnp.arange(GRID_W, dtype=F32), rows)
    n_f = MLA_ROPE // 4
    freqs = jnp.power(ROPE_THETA, -jnp.arange(n_f, dtype=F32) / n_f)
    ang = jnp.concatenate([r[:, None] * freqs, col[:, None] * freqs], axis=-1)
    cos, sin = jnp.cos(ang), jnp.sin(ang)
    cs = _rope_pad_layout(jnp.concatenate([cos, cos], axis=-1))
    sn = _rope_pad_layout(jnp.concatenate([-sin, sin], axis=-1))
    ident = _rope_pad_layout(jnp.ones((tm, MLA_ROPE), F32))
    return (jnp.concatenate([ident, cs], axis=0), jnp.concatenate([jnp.zeros((tm, LANES), F32), sn], axis=0))


def _block_diag_expand(w):
    per = MXU_DIM // ML_QKV_BLOCK
    nblk = ML_DI // MXU_DIM
    w5 = w.reshape(nblk, per, ML_QKV_BLOCK, 1, ML_QKV_BLOCK)
    eye = jnp.eye(per, dtype=w.dtype).reshape(1, per, 1, per, 1)
    return (w5 * eye).reshape(nblk, MXU_DIM, MXU_DIM).astype(BF16)


def _gate_layouts(gp):
    g3 = gp.reshape(T_TOK, 4, ML_NH)
    return jnp.transpose(g3, (2, 0, 1)), jnp.transpose(g3, (2, 1, 0))


def kernel(x_prompt, x_sample, state_mlstm_C, state_mlstm_n, state_mlstm_m, cache_mla, c, c_ctx, w_mod, b_mod, norm_g, ffn_w_in, ffn_w_out, ml_w_up, ml_conv_w, ml_conv_b, ml_w_qkv, ml_w_gate, ml_b_gate, ml_head_g, ml_skip, ml_w_down, mla_w_in, mla_q_norm_g, mla_w_uq, mla_kv_norm_g, mla_w_ukv, mla_qk_g, mla_w_o):
    x = jnp.concatenate([x_prompt.reshape(NP_TOK, D_MODEL), x_sample.reshape(NS_TOK, D_MODEL)], axis=0)
    cvecs = jnp.concatenate([c_ctx[None, :], c, jnp.zeros((MOD_ROWS - 1 - DEC_BATCH, D_MODEL), F32)], axis=0)
    mod = _mod_call(cvecs, w_mod, b_mod)
    cs_tab, sn_tab = _rope_tables(TM_MLA)

    n0 = state_mlstm_n.reshape(DEC_BATCH, N_ML, 2, ML_NH, 1, ML_DH)
    m0 = jnp.broadcast_to(state_mlstm_m[..., None, None], (DEC_BATCH, N_ML, 2, ML_NH, 1, LANES))

    new_c, new_n, new_m, new_kv = [], [], [], []
    for i in range(DEPTH):
        j = i // 2
        mod_l = mod[i]
        g3 = norm_g[i]
        x = _ffn_call(x, mod_l, g3, ffn_w_in[i, 0].astype(BF16), ffn_w_out[i, 0].astype(BF16), 0)
        if i % 2 == 0:
            wg = jnp.concatenate([ml_w_gate[j, 0], ml_w_gate[j, 1]], axis=-1).astype(BF16)
            bg = ml_b_gate[j].reshape(1, 4 * ML_NH)
            q, k, v, xc, z, gp = _ml_pre_call(
                x, mod_l, g3, ml_w_up[j].astype(BF16), ml_conv_w[j], ml_conv_b[j].reshape(1, ML_DI),
                _block_diag_expand(ml_w_qkv[j, 0]), _block_diag_expand(ml_w_qkv[j, 1]),
                _block_diag_expand(ml_w_qkv[j, 2]), wg, bg)
            gpc, gpr = _gate_layouts(gp)
            hp, c_new, n_new, m_new = _scan_prompt_call(q, k, v, gpc, gpr)
            hf, hb = _scan_sample_call(q, k, v, gpc, gpr, state_mlstm_C, n0, m0, j)
            x = _ml_out_call(x, hp, hf, hb, xc, z, mod_l, ml_head_g[j].reshape(1, ML_DI),
                             ml_skip[j].reshape(1, ML_DI), ml_w_down[j].astype(BF16))
            new_c.append(c_new)
            new_n.append(n_new.reshape(BATCH, 2, ML_NH, ML_DH))
            new_m.append(m_new[:, :, :, 0, 0])
        else:
            w_in = mla_w_in[j]
            w_in_ext = jnp.concatenate(
                [w_in[:, :MLA_Q_LORA + MLA_KV_LORA], _rope_pad_layout(w_in[:, MLA_Q_LORA + MLA_KV_LORA:]),
                 w_in[:, MLA_Q_LORA + MLA_KV_LORA:]], axis=-1).astype(BF16)
            wuq = mla_w_uq[j].reshape(MLA_Q_LORA, MLA_NH, MLA_QK)
            wuq = jnp.concatenate([wuq[..., :MLA_NOPE], _rope_pad_layout(wuq[..., MLA_NOPE:])], axis=-1)
            wuq = wuq.reshape(MLA_Q_LORA, MLA_NH * HEAD_SLOT).astype(BF16)
            wukv = mla_w_ukv[j].reshape(MLA_KV_LORA, MLA_NH, MLA_NOPE + MLA_DV)
            wuk = wukv[..., :MLA_NOPE].reshape(MLA_KV_LORA, MLA_NH * MLA_NOPE).astype(BF16)
            wuv = wukv[..., MLA_NOPE:].reshape(MLA_KV_LORA, MLA_NH * MLA_DV).astype(BF16)
            qk_g = mla_qk_g[j]
            qkg = jnp.concatenate([qk_g[:, :MLA_NOPE], _rope_pad_layout(qk_g[:, MLA_NOPE:])], axis=-1)
            qg, kg = qkg[0:1], qkg[1:2]
            q, k, v, entry = _mla_pre_call(
                x, mod_l, g3, w_in_ext, mla_q_norm_g[j].reshape(1, MLA_Q_LORA), wuq,
                mla_kv_norm_g[j].reshape(1, MLA_KV_LORA), wuk, wuv, qg, kg, cs_tab, sn_tab)
            ctx = cache_mla[:, j].reshape(DEC_BATCH * PAST_LEN, MLA_KV_LORA + MLA_ROPE)
            k_ctx, v_ctx = _mla_ctx_call(ctx[:, :MLA_KV_LORA], _rope_pad_layout(ctx[:, MLA_KV_LORA:]), wuk, wuv, kg)
            k_all = jnp.concatenate([k[NP_TOK:].reshape(DEC_BATCH, DEC_SEQ, -1),
                                     k_ctx.reshape(DEC_BATCH, PAST_LEN, -1)], axis=1)
            v_all = jnp.concatenate([v[NP_TOK:].reshape(DEC_BATCH, DEC_SEQ, -1),
                                     v_ctx.reshape(DEC_BATCH, PAST_LEN, -1)], axis=1)
            o_p = _attn_prompt_call(q, k, v)
            o_s = _attn_sample_call(q, k_all, v_all)
            x = _mla_out_call(x, o_p, o_s, mod_l, mla_w_o[j].astype(BF16))
            new_kv.append(entry[:NP_TOK].reshape(BATCH, SEQ, MLA_KV_LORA + MLA_ROPE))
        x = _ffn_call(x, mod_l, g3, ffn_w_in[i, 1].astype(BF16), ffn_w_out[i, 1].astype(BF16), 2)

    return (x[:NP_TOK].reshape(BATCH, SEQ, D_MODEL), x[NP_TOK:].reshape(DEC_BATCH, DEC_SEQ, D_MODEL),
            jnp.stack(new_c, axis=1), jnp.stack(new_n, axis=1), jnp.stack(new_m, axis=1),
            jnp.stack(new_kv, axis=1))
```
